```python
import math
import jax, jax.numpy as jnp
from jax import lax
import numpy as np

D_MODEL = 1024
BATCH = 32
SEQ = 256
DEPTH = 2
DEC_BATCH = 2
DEC_SEQ = 4096
PAST_LEN = 256

GRID_W = 64
MIX = D_MODEL
D_HY = D_MODEL // 4
D_NA = D_MODEL // 2
D_S5 = D_MODEL // 4
HYENA_ORDER = 2
HY_SHORT_CONV = 3
HY_BANDS = 16
HY_FEAT = 1 + 2 * HY_BANDS
HY_FILTER_HIDDEN = 64
HY_DECAY_SLOW = math.log(100.0) / 1.5
HY_DECAY_FAST = math.log(100.0) / 0.3
NA_HEAD_DIM = 64
NA_HEADS = D_NA // NA_HEAD_DIM
WIN_R_MAX = 8
WIN_C = 16
Q_COLS = 16
BAND = 32
N_COL_BLOCKS = GRID_W // Q_COLS
ROPE_BASE = 10000.0
ROPE_PAIRS = NA_HEAD_DIM // 4
Q_BLOCK = 128
S5_CH = 16
S5_GROUPS = D_S5 // S5_CH
S5_STATE = 64
S5_DT_MIN = 1e-3
S5_DT_MAX = 1e-1
IN_WIDTH = 3 * D_HY + 3 * D_NA + D_S5 + MIX
EPS = 1e-6
F32 = jnp.float32

kernel_name = 'hyena_natten_s5_prefix_dit_step'


def _rms_norm(x, g):
    x32 = x.astype(F32)
    y = x32 * lax.rsqrt(jnp.mean(x32 * x32, axis=-1, keepdims=True) + EPS)
    return (y * g.astype(F32)).astype(x.dtype)


def _pre(x, mod, norm_g, in_w):
    shift, scale, gate = jnp.split(mod, 3, axis=-1)
    h = _rms_norm(x, norm_g) * (1 + scale) + shift
    return h @ in_w, gate


def _split_proj(z):
    o1 = 3 * D_HY
    o2 = o1 + D_NA
    o3 = o2 + D_NA
    o4 = o3 + D_NA
    o5 = o4 + D_S5
    return jnp.split(z, [o1, o2, o3, o4, o5], axis=-1)


def _heads(t):
    b, l, _ = t.shape
    return t.reshape(b, l, NA_HEADS, NA_HEAD_DIM).transpose(0, 2, 1, 3)


def _merge(t):
    b, h, l, d = t.shape
    return t.transpose(0, 2, 1, 3).reshape(b, l, h * d)


def _post(x, y_hy, y_na, y_s5, g, res_gate, out_w):
    y = jnp.concatenate([y_hy, y_na, y_s5], axis=-1) * jax.nn.silu(g)
    return x + res_gate * (y @ out_w)


def _hyena_filter_spectra(L, f_w1, f_b1, f_w2, f_b2, f_freq, f_w3, decay):
    t = jnp.arange(L, dtype=F32) / L
    ang = 2.0 * jnp.pi * t[:, None] * jnp.arange(1, HY_BANDS + 1, dtype=F32)
    feat = jnp.concatenate([t[:, None], jnp.cos(ang), jnp.sin(ang)], axis=-1)
    freq = f_freq.astype(F32)
    h = jnp.sin(freq * (feat @ f_w1.astype(F32) + f_b1.astype(F32)))
    h = jnp.sin(freq * (h @ f_w2.astype(F32) + f_b2.astype(F32)))
    h = (h @ f_w3.astype(F32)) * jnp.exp(-t[:, None] * jnp.abs(decay.astype(F32)))
    h = h.reshape(L, HYENA_ORDER, 2, D_HY)
    h_fwd, h_bwd = h[:, :, 0], h[:, :, 1]
    taps = jnp.concatenate([h_fwd, jnp.zeros((1, HYENA_ORDER, D_HY), F32), h_bwd[:0:-1]], axis=0)
    taps = taps / jnp.sum(jnp.abs(taps), axis=0, keepdims=True)
    return jnp.fft.rfft(taps, axis=0)


def _fft_conv(u, spec):
    L = u.shape[1]
    U = jnp.fft.rfft(u, n=2 * L, axis=1)
    return jnp.fft.irfft(U * spec[None], n=2 * L, axis=1)[:, :L]


def _hyena(zh, lp):
    L = zh.shape[1]
    w = lp['hy_conv_w']
    pad = HY_SHORT_CONV // 2
    zp = jnp.pad(zh, ((0, 0), (pad, pad), (0, 0)))
    zc = lp['hy_conv_b'] + sum(zp[:, j:j + L] * w[j] for j in range(HY_SHORT_CONV))
    v, x1, x2 = jnp.split(zc.astype(F32), 3, axis=-1)
    spec = _hyena_filter_spectra(L, lp['hy_f_w1'], lp['hy_f_b1'], lp['hy_f_w2'], lp['hy_f_b2'],
                                 lp['hy_f_freq'], lp['hy_f_w3'], lp['hy_decay'])
    bias = lp['hy_bias'].astype(F32)
    z = v
    for o, gate in enumerate((x1, x2)):
        z = gate * (_fft_conv(z, spec[:, o]) + bias[o] * z)
    return z.astype(zh.dtype)


def _rotate(x, ang):
    cos = jnp.cos(ang).astype(x.dtype)
    sin = jnp.sin(ang).astype(x.dtype)
    x1, x2 = jnp.split(x, 2, axis=-1)
    return jnp.concatenate([x1 * cos - x2 * sin, x2 * cos + x1 * sin], axis=-1)


def _axial_rope(x):
    T = x.shape[-2]
    t = jnp.arange(T)
    row = (t // GRID_W).astype(F32)
    col = (t % GRID_W).astype(F32)
    inv = ROPE_BASE ** (-jnp.arange(ROPE_PAIRS, dtype=F32) / ROPE_PAIRS)
    xr, xc = jnp.split(x, 2, axis=-1)
    return jnp.concatenate([_rotate(xr, row[:, None] * inv), _rotate(xc, col[:, None] * inv)], axis=-1)


def _dense_attention(q, k, v):
    b, h, L, d = q.shape
    scale = d ** -0.5
    qb = jnp.moveaxis(q.reshape(b, h, L // Q_BLOCK, Q_BLOCK, d), 2, 0)

    def blk(qi):
        s = jnp.einsum('bhqd,bhkd->bhqk', qi, k, preferred_element_type=F32) * scale
        p = jax.nn.softmax(s, axis=-1).astype(v.dtype)
        return jnp.einsum('bhqk,bhkd->bhqd', p, v)

    o = lax.map(blk, qb)
    return jnp.moveaxis(o, 0, 2).reshape(b, h, L, d)


def _band_tables():
    qcol = np.arange(GRID_W).reshape(N_COL_BLOCKS, Q_COLS)
    cstart = np.clip(qcol - WIN_C // 2, 0, GRID_W - WIN_C)
    bstart = np.clip(qcol[:, 0] - WIN_C // 2, 0, GRID_W - BAND)
    bcol = bstart[:, None] + np.arange(BAND)
    off = bcol[:, None, :] - qcol[:, :, None]
    valid = (bcol[:, None, :] >= cstart[:, :, None]) & (bcol[:, None, :] < cstart[:, :, None] + WIN_C)
    col_idx = np.clip(off, -(WIN_C - 1), WIN_C - 1) + (WIN_C - 1)
    return bcol, col_idx, valid


def _na_latent(q_rot, q_raw, k_rot, v, k_ctx, v_ctx, rpb):
    b, h, T, d = q_rot.shape
    R = T // GRID_W
    wr = min(WIN_R_MAX, R)
    scale = d ** -0.5
    bcol, col_idx, valid = _band_tables()
    kg = k_rot.reshape(b, h, R, GRID_W, d)
    vg = v.reshape(b, h, R, GRID_W, d)

    def per_row(a):
        return jnp.moveaxis(a.reshape(b, h, R, N_COL_BLOCKS, Q_COLS, d), 2, 0)

    valid_j = jnp.asarray(valid)[:, :, None, :]
    rpb_cols = rpb.astype(F32)[:, :, col_idx]

    def row_block(args):
        r, qr, qc = args
        rs = jnp.clip(r - wr // 2, 0, R - wr)
        kb = lax.dynamic_slice_in_dim(kg, rs, wr, axis=2)[:, :, :, bcol]
        vb = lax.dynamic_slice_in_dim(vg, rs, wr, axis=2)[:, :, :, bcol]
        s_win = jnp.einsum('bhnqd,bhrnkd->bhnqrk', qr, kb, preferred_element_type=F32) * scale
        row_idx = rs + jnp.arange(wr) - r + (WIN_R_MAX - 1)
        bias = jnp.take(rpb_cols, row_idx, axis=1).transpose(0, 2, 3, 1, 4)
        s_win = jnp.where(valid_j, s_win + bias, -jnp.inf)
        s_win = s_win.reshape(b, h, N_COL_BLOCKS, Q_COLS, wr * BAND)
        s_ctx = jnp.einsum('bhnqd,bhld->bhnql', qc, k_ctx, preferred_element_type=F32) * scale
        p = jax.nn.softmax(jnp.concatenate([s_win, s_ctx], axis=-1), axis=-1).astype(v.dtype)
        p_win = p[..., :wr * BAND].reshape(b, h, N_COL_BLOCKS, Q_COLS, wr, BAND)
        p_ctx = p[..., wr * BAND:]
        o = (jnp.einsum('bhnqrk,bhrnkd->bhnqd', p_win, vb)
             + jnp.einsum('bhnql,bhld->bhnqd', p_ctx, v_ctx))
        return o.reshape(b, h, GRID_W, d)

    out = lax.map(row_block, (jnp.arange(R), per_row(q_rot), per_row(q_raw)))
    return jnp.moveaxis(out, 0, 2).reshape(b, h, T, d)


def _diag_scan(lam_bar, bu, s0):
    a = jnp.broadcast_to(lam_bar, bu.shape)

    def combine(left, right):
        return left[0] * right[0], right[0] * left[1] + right[1]

    a_cum, s = lax.associative_scan(combine, (a, bu), axis=1)
    return s + a_cum * s0[:, None]


def _s5(u, s0_re, s0_im, lp):
    b, L, _ = u.shape
    u32 = u.astype(F32)
    ug = u32.reshape(b, L, S5_GROUPS, S5_CH).astype(jnp.complex64)
    lam = lax.complex(lp['s5_a_re'].astype(F32), lp['s5_a_im'].astype(F32))
    dt = jnp.exp(lp['s5_log_dt'].astype(F32))[..., None]
    lam_bar = jnp.exp(lam * dt)
    b_mat = lax.complex(lp['s5_b_re'].astype(F32), lp['s5_b_im'].astype(F32))
    b_bar = ((lam_bar - 1.0) / lam)[..., None] * b_mat
    c_mat = lax.complex(lp['s5_c_re'].astype(F32), lp['s5_c_im'].astype(F32))
    s0 = lax.complex(s0_re.astype(F32), s0_im.astype(F32))
    bu = jnp.einsum('blgc,egpc->eblgp', ug, b_bar)
    s_f = _diag_scan(lam_bar[0], bu[0], s0[:, 0])
    s_b = _diag_scan(lam_bar[1], bu[1][:, ::-1], s0[:, 1])[:, ::-1]
    y = (jnp.einsum('blgp,gcp->blgc', s_f, c_mat[0])
         + jnp.einsum('blgp,gcp->blgc', s_b, c_mat[1])).real
    y = y.reshape(b, L, D_S5) + lp['s5_d'].astype(F32) * u32
    y = jax.nn.gelu(y)
    y = y * jax.nn.sigmoid(y @ lp['s5_glu_w'].astype(F32) + lp['s5_glu_b'].astype(F32))
    fin = jnp.stack([s_f[:, -1], s_b[:, 0]], axis=1)
    return y.astype(u.dtype), fin.real.astype(u.dtype), fin.imag.astype(u.dtype)


def _context_layer(x, mod, lp):
    b = x.shape[0]
    z, res_gate = _pre(x, mod, lp['norm_g'], lp['in_w'])
    zh, q, k, v, su, g = _split_proj(z)
    y_hy = _hyena(zh, lp)
    qh, kh, vh = _heads(q), _heads(k), _heads(v)
    y_na = _merge(_dense_attention(qh, kh, vh))
    s0 = jnp.zeros((b, 2, S5_GROUPS, S5_STATE), F32)
    y_s5, fin_re, fin_im = _s5(su, s0, s0, lp)
    x = _post(x, y_hy, y_na, y_s5, g, res_gate, lp['out_w'])
    return x, kh, vh, fin_re, fin_im


def _latent_layer(x, mod, k_ctx, v_ctx, s0_re, s0_im, lp):
    z, res_gate = _pre(x, mod, lp['norm_g'], lp['in_w'])
    zh, q, k, v, su, g = _split_proj(z)
    y_hy = _hyena(zh, lp)
    qh, kh, vh = _heads(q), _heads(k), _heads(v)
    y_na = _merge(_na_latent(_axial_rope(qh), qh, _axial_rope(kh), vh, k_ctx, v_ctx, lp['na_rpb']))
    y_s5, _, _ = _s5(su, s0_re, s0_im, lp)
    return _post(x, y_hy, y_na, y_s5, g, res_gate, lp['out_w'])


def setup_inputs(seed: int = 0) -> dict:
    key = jax.random.key(seed)
    ks = jax.random.split(key, 35)

    def nrm(i, shape, s=1.0):
        return jax.random.normal(ks[i], shape, F32) * s

    decay_base = jnp.concatenate(
        [jnp.linspace(HY_DECAY_SLOW, HY_DECAY_FAST, D_HY, dtype=F32)] * (HYENA_ORDER * 2))
    return {
        'x_prompt': nrm(0, (BATCH, SEQ, D_MODEL)),
        'x_sample': nrm(1, (DEC_BATCH, DEC_SEQ, D_MODEL)),
        'cache_k': nrm(2, (DEC_BATCH, DEPTH, NA_HEADS, PAST_LEN, NA_HEAD_DIM)),
        'cache_v': nrm(3, (DEC_BATCH, DEPTH, NA_HEADS, PAST_LEN, NA_HEAD_DIM)),
        'state_s5_re': nrm(4, (DEC_BATCH, DEPTH, 2, S5_GROUPS, S5_STATE), 0.1),
        'state_s5_im': nrm(5, (DEC_BATCH, DEPTH, 2, S5_GROUPS, S5_STATE), 0.1),
        'c': nrm(6, (DEC_BATCH, D_MODEL)),
        'c_ctx': nrm(7, (D_MODEL,)),
        'norm_g': 1.0 + nrm(8, (DEPTH, D_MODEL), 0.02),
        'ada_w': nrm(9, (DEPTH, D_MODEL, 3 * D_MODEL), D_MODEL ** -0.5),
        'ada_b': nrm(10, (DEPTH, 3 * D_MODEL), 0.02),
        'in_w': nrm(11, (DEPTH, D_MODEL, IN_WIDTH), D_MODEL ** -0.5),
        'out_w': nrm(12, (DEPTH, MIX, D_MODEL), MIX ** -0.5),
        'hy_conv_w': nrm(13, (DEPTH, HY_SHORT_CONV, 3 * D_HY), HY_SHORT_CONV ** -0.5),
        'hy_conv_b': nrm(14, (DEPTH, 3 * D_HY), 0.02),
        'hy_f_w1': nrm(15, (DEPTH, HY_FEAT, HY_FILTER_HIDDEN), HY_FEAT ** -0.5),
        'hy_f_b1': nrm(16, (DEPTH, HY_FILTER_HIDDEN), 0.1),
        'hy_f_w2': nrm(17, (DEPTH, HY_FILTER_HIDDEN, HY_FILTER_HIDDEN), HY_FILTER_HIDDEN ** -0.5),
        'hy_f_b2': nrm(18, (DEPTH, HY_FILTER_HIDDEN), 0.1),
        'hy_f_freq': 1.0 + nrm(19, (DEPTH, HY_FILTER_HIDDEN), 0.02),
        'hy_f_w3': nrm(20, (DEPTH, HY_FILTER_HIDDEN, HYENA_ORDER * 2 * D_HY), HY_FILTER_HIDDEN ** -0.5),
        'hy_decay': decay_base[None] + nrm(21, (DEPTH, HYENA_ORDER * 2 * D_HY), 0.05),
        'hy_bias': nrm(22, (DEPTH, HYENA_ORDER, D_HY)),
        'na_rpb': nrm(23, (DEPTH, NA_HEADS, 2 * WIN_R_MAX - 1, 2 * WIN_C - 1), 0.1),
        's5_a_re': -0.5 + nrm(24, (DEPTH, 2, S5_GROUPS, S5_STATE), 0.01),
        's5_a_im': jnp.pi * jnp.arange(S5_STATE, dtype=F32) + nrm(25, (DEPTH, 2, S5_GROUPS, S5_STATE), 0.01),
        's5_log_dt': jax.random.uniform(ks[26], (DEPTH, 2, S5_GROUPS), F32,
                                        minval=math.log(S5_DT_MIN), maxval=math.log(S5_DT_MAX)),
        's5_b_re': nrm(27, (DEPTH, 2, S5_GROUPS, S5_STATE, S5_CH), (2 * S5_CH) ** -0.5),
        's5_b_im': nrm(28, (DEPTH, 2, S5_GROUPS, S5_STATE, S5_CH), (2 * S5_CH) ** -0.5),
        's5_c_re': nrm(29, (DEPTH, 2, S5_GROUPS, S5_CH, S5_STATE), S5_STATE ** -0.5),
        's5_c_im': nrm(30, (DEPTH, 2, S5_GROUPS, S5_CH, S5_STATE), S5_STATE ** -0.5),
        's5_d': nrm(31, (DEPTH, D_S5)),
        's5_glu_w': nrm(32, (DEPTH, D_S5, D_S5), D_S5 ** -0.5),
        's5_glu_b': nrm(33, (DEPTH, D_S5), 0.02),
        'final_norm_g': 1.0 + nrm(34, (D_MODEL,), 0.02),
    }


def reference(x_prompt, x_sample, cache_k, cache_v, state_s5_re, state_s5_im, c, c_ctx,
              norm_g, ada_w, ada_b, in_w, out_w, hy_conv_w, hy_conv_b, hy_f_w1, hy_f_b1,
              hy_f_w2, hy_f_b2, hy_f_freq, hy_f_w3, hy_decay, hy_bias, na_rpb,
              s5_a_re, s5_a_im, s5_log_dt, s5_b_re, s5_b_im, s5_c_re, s5_c_im, s5_d,
              s5_glu_w, s5_glu_b, final_norm_g):
    xp = x_prompt
    xs = x_sample
    new_k, new_v, new_re, new_im = [], [], [], []
    for l in range(DEPTH):
        lp = {
            'norm_g': norm_g[l], 'in_w': in_w[l], 'out_w': out_w[l],
            'hy_conv_w': hy_conv_w[l], 'hy_conv_b': hy_conv_b[l],
            'hy_f_w1': hy_f_w1[l], 'hy_f_b1': hy_f_b1[l], 'hy_f_w2': hy_f_w2[l], 'hy_f_b2': hy_f_b2[l],
            'hy_f_freq': hy_f_freq[l], 'hy_f_w3': hy_f_w3[l], 'hy_decay': hy_decay[l], 'hy_bias': hy_bias[l],
            'na_rpb': na_rpb[l],
            's5_a_re': s5_a_re[l], 's5_a_im': s5_a_im[l], 's5_log_dt': s5_log_dt[l],
            's5_b_re': s5_b_re[l], 's5_b_im': s5_b_im[l], 's5_c_re': s5_c_re[l], 's5_c_im': s5_c_im[l],
            's5_d': s5_d[l], 's5_glu_w': s5_glu_w[l], 's5_glu_b': s5_glu_b[l],
        }
        mod_ctx = (jax.nn.silu(c_ctx) @ ada_w[l] + ada_b[l])[None, None, :]
        xp, kc, vc, fre, fim = _context_layer(xp, mod_ctx, lp)
        new_k.append(kc)
        new_v.append(vc)
        new_re.append(fre)
        new_im.append(fim)
        mod_lat = (jax.nn.silu(c) @ ada_w[l] + ada_b[l])[:, None, :]
        xs = _latent_layer(xs, mod_lat, cache_k[:, l], cache_v[:, l],
                           state_s5_re[:, l], state_s5_im[:, l], lp)
    y_prompt = _rms_norm(xp, final_norm_g)
    y_sample = _rms_norm(xs, final_norm_g)
    new_cache_k = jnp.stack(new_k, axis=1)
    new_cache_v = jnp.stack(new_v, axis=1)
    new_state_s5_re = jnp.stack(new_re, axis=1)
    new_state_s5_im = jnp.stack(new_im, axis=1)
    return (y_prompt, y_sample, new_cache_k, new_cache_v, new_state_s5_re, new_state_s5_im)
```

```python
import functools
import math

import numpy as np
import jax
import jax.numpy as jnp
from jax import lax
from jax.experimental import pallas as pl
from jax.experimental.pallas import tpu as pltpu

F32 = jnp.float32
BF16 = jnp.bfloat16

D_MODEL = 1024
DEPTH = 2
D_HY = 256
D_NA = 512
D_S5 = 256
MIX = 1024
IN_WIDTH = 3 * D_HY + 3 * D_NA + D_S5 + MIX
EPS = 1e-6
GRID_W = 64
HEAD_DIM = 64
NA_HEADS = 8
WIN_R = 8
WIN_C = 16
ROPE_BASE = 10000.0
ROPE_PAIRS = 16
HY_BANDS = 16
HY_FEAT = 1 + 2 * HY_BANDS
S5_GROUPS = 16
S5_CH = 16
S5_STATE = 64
S5_Q = 16
NEG_BIG = -1e30

LANES = 128
VMEM_LIMIT_BYTES = 56 * 1024 * 1024


def _params(*semantics):
    return pltpu.CompilerParams(dimension_semantics=semantics, vmem_limit_bytes=VMEM_LIMIT_BYTES)


def _split_bf16(a):
    hi = a.astype(BF16)
    lo = (a - hi.astype(F32)).astype(BF16)
    return hi, lo


def _dot(a, b):
    return jnp.dot(a, b, preferred_element_type=F32)


def _dot_nt(a, b):
    return lax.dot_general(a, b, (((1,), (1,)), ((), ())), preferred_element_type=F32)


def _dot3(a, b):
    a_hi, a_lo = _split_bf16(a)
    b_hi, b_lo = _split_bf16(b)
    return _dot(a_hi, b_hi) + _dot(a_hi, b_lo) + _dot(a_lo, b_hi)


def _dot2(a, w_bf16):
    a_hi, a_lo = _split_bf16(a)
    return _dot(a_hi, w_bf16) + _dot(a_lo, w_bf16)


def _silu(x):
    return x / (1.0 + jnp.exp(-x))


MOD_TN = 512


def _mod_kernel(cond_ref, w_ref, b_ref, o_ref):
    o_ref[...] = _dot3(_silu(cond_ref[...]), w_ref[...]) + b_ref[...]


def _modulation(cond, ada_w, ada_b):
    return pl.pallas_call(
        _mod_kernel,
        grid=(DEPTH, 3 * D_MODEL // MOD_TN),
        in_specs=[
            pl.BlockSpec((8, D_MODEL), lambda l, j: (0, 0)),
            pl.BlockSpec((None, D_MODEL, MOD_TN), lambda l, j: (l, 0, j)),
            pl.BlockSpec((None, 1, MOD_TN), lambda l, j: (l, 0, j)),
        ],
        out_specs=pl.BlockSpec((None, 8, MOD_TN), lambda l, j: (l, 0, j)),
        out_shape=jax.ShapeDtypeStruct((DEPTH, 8, 3 * D_MODEL), F32),
        compiler_params=_params("parallel", "parallel"),
        name="modulation",
    )(cond, ada_w, ada_b.reshape(DEPTH, 1, 3 * D_MODEL))


ROW_TM = 512
IN_CHUNK = 256
_IN_SPLITS = (3 * D_HY, 3 * D_NA, LANES, LANES, MIX)


def _in_proj_kernel(x_ref, g_ref, shift_ref, scale_ref, w_ref, zh_ref, qkv_ref, sua_ref, sub_ref, gp_ref):
    x = x_ref[...]
    y = x * lax.rsqrt(jnp.mean(x * x, axis=-1, keepdims=True) + EPS) * g_ref[...]
    h = (y * (1.0 + scale_ref[...]) + shift_ref[...]).astype(BF16)
    col = 0
    for o_ref, width in zip((zh_ref, qkv_ref, sua_ref, sub_ref, gp_ref), _IN_SPLITS):
        step = min(width, IN_CHUNK)
        for c in range(0, width, step):
            o_ref[:, c:c + step] = _dot(h, w_ref[:, col + c:col + c + step])
        col += width


def _mod_row(i, mod_base, tiles_per_seq):
    return mod_base + i // tiles_per_seq


def _in_proj(x, mod3, norm_g, w_bf16, mod_base, seq_len):
    n = x.shape[0]
    grp = functools.partial(_mod_row, mod_base=mod_base, tiles_per_seq=seq_len // ROW_TM if mod_base else n)
    row = lambda width: pl.BlockSpec((ROW_TM, width), lambda i: (i, 0))
    return pl.pallas_call(
        _in_proj_kernel,
        grid=(n // ROW_TM,),
        in_specs=[
            row(D_MODEL),
            pl.BlockSpec((1, D_MODEL), lambda i: (0, 0)),
            pl.BlockSpec((None, 1, D_MODEL), lambda i: (grp(i), 0, 0)),
            pl.BlockSpec((None, 1, D_MODEL), lambda i: (grp(i), 0, 1)),
            pl.BlockSpec((D_MODEL, IN_WIDTH), lambda i: (0, 0)),
        ],
        out_specs=[row(w) for w in _IN_SPLITS],
        out_shape=[jax.ShapeDtypeStruct((n, w), F32) for w in _IN_SPLITS],
        compiler_params=_params("parallel"),
        name="in_proj",
    )(x, norm_g.reshape(1, D_MODEL), mod3, mod3, w_bf16)


def _out_proj_kernel(x_ref, yh_ref, yn_ref, ys_ref, gp_ref, gate_ref, w_ref, fg_ref, o_ref, *, final):
    sg = _silu(gp_ref[...])
    acc = _dot((yh_ref[...] * sg[:, :D_HY]).astype(BF16), w_ref[:D_HY, :])
    acc += _dot((yn_ref[...] * sg[:, D_HY:D_HY + D_NA]).astype(BF16), w_ref[D_HY:D_HY + D_NA, :])
    acc += _dot((ys_ref[...] * sg[:, D_HY + D_NA:]).astype(BF16), w_ref[D_HY + D_NA:, :])
    out = x_ref[...] + gate_ref[...] * acc
    if final:
        out = out * lax.rsqrt(jnp.mean(out * out, axis=-1, keepdims=True) + EPS) * fg_ref[...]
    o_ref[...] = out


def _out_proj(x, y_hy, y_na, y_s5, gp, mod3, w_bf16, final_g, mod_base, seq_len, final):
    n = x.shape[0]
    grp = functools.partial(_mod_row, mod_base=mod_base, tiles_per_seq=seq_len // ROW_TM if mod_base else n)
    row = lambda width: pl.BlockSpec((ROW_TM, width), lambda i: (i, 0))
    return pl.pallas_call(
        functools.partial(_out_proj_kernel, final=final),
        grid=(n // ROW_TM,),
        in_specs=[
            row(D_MODEL), row(D_HY), row(D_NA), row(D_S5), row(MIX),
            pl.BlockSpec((None, 1, D_MODEL), lambda i: (grp(i), 0, 2)),
            pl.BlockSpec((MIX, D_MODEL), lambda i: (0, 0)),
            pl.BlockSpec((1, D_MODEL), lambda i: (0, 0)),
        ],
        out_specs=row(D_MODEL),
        out_shape=jax.ShapeDtypeStruct((n, D_MODEL), F32),
        compiler_params=_params("parallel"),
        name="out_proj_final" if final else "out_proj",
    )(x, y_hy, y_na, y_s5, gp, mod3, w_bf16, final_g.reshape(1, D_MODEL))


PAIR_W = 2 * D_HY
HY_TL = 512


def _hy_prep_kernel(c0, p0, n0, c1, p1, n1, w_ref, b_ref, v_ref, x1_ref, x2_ref, *, tl, nblk):
    i = pl.program_id(1)
    rows = lax.broadcasted_iota(jnp.int32, (tl, 1), 0)
    w = w_ref[...]
    for q, (cur_ref, prev_ref, next_ref) in enumerate(((c0, p0, n0), (c1, p1, n1))):
        cur = cur_ref[...]
        prev_row = jnp.where(i > 0, prev_ref[7:8, :], 0.0)
        next_row = jnp.where(i < nblk - 1, next_ref[0:1, :], 0.0)
        up = jnp.where(rows == 0, prev_row, pltpu.roll(cur, 1, 0))
        dn = jnp.where(rows == tl - 1, next_row, pltpu.roll(cur, tl - 1, 0))
        zc = b_ref[...] + up * w[0:1, :] + cur * w[1:2, :] + dn * w[2:3, :]
        lo, hi = q * D_HY, (q + 1) * D_HY
        v_ref[:, lo:hi] = zc[:, :D_HY]
        x1_ref[:, lo:hi] = zc[:, D_HY:2 * D_HY]
        x2_ref[:, lo:hi] = zc[:, 2 * D_HY:]


def _hy_prep(zh3, conv_w, conv_b):
    s, l, _ = zh3.shape
    tl = min(l, HY_TL)
    nblk = l // tl
    width = 3 * D_HY
    specs = []
    for q in range(2):
        specs += [
            pl.BlockSpec((None, tl, width), lambda p, i, q=q: (2 * p + q, i, 0)),
            pl.BlockSpec((None, 8, width), lambda p, i, q=q: (2 * p + q, jnp.maximum(i * (tl // 8) - 1, 0), 0)),
            pl.BlockSpec((None, 8, width), lambda p, i, q=q: (2 * p + q, jnp.minimum((i + 1) * (tl // 8), l // 8 - 1), 0)),
        ]
    out = pl.BlockSpec((None, tl, PAIR_W), lambda p, i: (p, i, 0))
    return pl.pallas_call(
        functools.partial(_hy_prep_kernel, tl=tl, nblk=nblk),
        grid=(s // 2, nblk),
        in_specs=specs + [pl.BlockSpec((3, width), lambda p, i: (0, 0)), pl.BlockSpec((1, width), lambda p, i: (0, 0))],
        out_specs=[out, out, out],
        out_shape=[jax.ShapeDtypeStruct((s // 2, l, PAIR_W), F32)] * 3,
        compiler_params=_params("parallel", "parallel"),
        name="hy_prep",
    )(zh3, zh3, zh3, zh3, zh3, zh3, conv_w, conv_b.reshape(1, width))


def _hy_gate_kernel(x_ref, c_ref, z_ref, b_ref, o_ref, *, final):
    res = x_ref[...] * (c_ref[...] + b_ref[...] * z_ref[...])
    if final:
        o_ref[0] = res[:, :D_HY]
        o_ref[1] = res[:, D_HY:]
    else:
        o_ref[...] = res


def _hy_gate(xg, conv, z, bias, final):
    p, l, _ = xg.shape
    tl = min(l, HY_TL)
    blk = pl.BlockSpec((None, tl, PAIR_W), lambda a, i: (a, i, 0))
    if final:
        out_spec = pl.BlockSpec((2, tl, D_HY), lambda a, i: (a, i, 0))
        out_shape = jax.ShapeDtypeStruct((2 * p, l, D_HY), F32)
    else:
        out_spec, out_shape = blk, jax.ShapeDtypeStruct((p, l, PAIR_W), F32)
    return pl.pallas_call(
        functools.partial(_hy_gate_kernel, final=final),
        grid=(p, l // tl),
        in_specs=[blk, blk, blk, pl.BlockSpec((1, PAIR_W), lambda a, i: (0, 0))],
        out_specs=out_spec,
        out_shape=out_shape,
        compiler_params=_params("parallel", "parallel"),
        name="hy_gate_final" if final else "hy_gate",
    )(xg, conv, z, jnp.concatenate([bias, bias]).reshape(1, PAIR_W))


HY_TR = 512
FEAT_PAD = 128


def _hy_filter_kernel(feat_ref, t_ref, sel_ref, w1_ref, b1_ref, w2_ref, b2_ref, freq_ref, w3_ref, dec_ref,
                      taps_ref, norm_ref):
    freq = freq_ref[...]
    h = jnp.sin(freq * (_dot3(feat_ref[...], w1_ref[...]) + b1_ref[...]))
    h = jnp.sin(freq * (_dot3(h, w2_ref[...]) + b2_ref[...]))
    h = _dot3(h, w3_ref[...]) * jnp.exp(-t_ref[...] * jnp.abs(dec_ref[...]))
    sel = sel_ref[...]
    fwd = jnp.concatenate([h[:, 0:D_HY], h[:, 2 * D_HY:3 * D_HY]], axis=1)
    bwd = jnp.concatenate([h[:, D_HY:2 * D_HY], h[:, 3 * D_HY:]], axis=1)
    taps = jnp.where(sel > 0.0, fwd, jnp.where(sel < 0.0, bwd, 0.0))
    taps_ref[...] = taps

    @pl.when(pl.program_id(0) == 0)
    def _():
        norm_ref[...] = jnp.zeros_like(norm_ref)

    norm_ref[...] += jnp.sum(jnp.abs(taps), axis=0, keepdims=True)


def _filter_positions(l):
    n = 2 * l
    r = np.arange(n)
    j = np.where(r < l, r, n - r)
    t = jnp.asarray(j, F32) / l
    ang = 2.0 * jnp.pi * t[:, None] * jnp.arange(1, HY_BANDS + 1, dtype=F32)
    feat = jnp.concatenate([t[:, None], jnp.cos(ang), jnp.sin(ang), jnp.zeros((n, FEAT_PAD - HY_FEAT), F32)], axis=-1)
    sel = jnp.asarray(np.where(r < l, 1.0, np.where(r > l, -1.0, 0.0)), F32)
    return feat, t[:, None], sel[:, None]


def _hy_filter(l, w1, b1, w2, b2, freq, w3, decay):
    n = 2 * l
    feat, t, sel = _filter_positions(l)
    hid = w1.shape[1]
    w1p = jnp.concatenate([w1, jnp.zeros((FEAT_PAD - HY_FEAT, hid), F32)], axis=0)
    full = lambda shape: pl.BlockSpec(shape, lambda i: (0, 0))
    return pl.pallas_call(
        _hy_filter_kernel,
        grid=(n // HY_TR,),
        in_specs=[
            pl.BlockSpec((HY_TR, FEAT_PAD), lambda i: (i, 0)),
            pl.BlockSpec((HY_TR, 1), lambda i: (i, 0)),
            pl.BlockSpec((HY_TR, 1), lambda i: (i, 0)),
            full((FEAT_PAD, hid)), full((1, hid)), full((hid, hid)), full((1, hid)), full((1, hid)),
            full((hid, 4 * D_HY)), full((1, 4 * D_HY)),
        ],
        out_specs=[pl.BlockSpec((HY_TR, PAIR_W), lambda i: (i, 0)), full((1, PAIR_W))],
        out_shape=[jax.ShapeDtypeStruct((n, PAIR_W), F32), jax.ShapeDtypeStruct((1, PAIR_W), F32)],
        compiler_params=_params("arbitrary"),
        name="hy_filter",
    )(feat, t, sel, w1p, b1.reshape(1, hid), w2, b2.reshape(1, hid), freq.reshape(1, hid), w3, decay.reshape(1, -1))


def _cmatmul_kernel(*refs, nb, shared_g, shared_h, epilogue):
    g_ref, d_ref = refs[0], refs[1]
    o_ref = refs[-1]
    for j in range(nb):
        d = d_ref[j]
        dsw = jnp.concatenate([-d[:, D_HY:], d[:, :D_HY]], axis=1)
        g = g_ref[...] if shared_g else g_ref[j]
        out = _dot3(g, jnp.concatenate([d, dsw], axis=0))
        if epilogue == "mul":
            hsp = refs[2][0 if shared_h else j]
            hr, hi = hsp[:, :D_HY], hsp[:, D_HY:]
            o_r, o_i = out[:, :D_HY], out[:, D_HY:]
            out = jnp.concatenate([o_r * hr - o_i * hi, o_r * hi + o_i * hr], axis=1)
        elif epilogue == "scale":
            out = out * refs[2][...]
        o_ref[j] = out


def _cmatmul(g, data, nb, extra=None, epilogue=None, extra_period=None):
    b, k, _ = data.shape
    shared_g = g.ndim == 2
    m = g.shape[-2]
    assert g.shape[-1] == 2 * k and b % nb == 0
    steps = b // nb
    if shared_g:
        g_spec = pl.BlockSpec((m, 2 * k), lambda i: (0, 0))
    else:
        gsteps = g.shape[0] // nb
        g_spec = pl.BlockSpec((nb, m, 2 * k), lambda i: (i % gsteps, 0, 0))
    in_specs = [g_spec, pl.BlockSpec((nb, k, PAIR_W), lambda i: (i, 0, 0))]
    args = [g, data]
    shared_h = epilogue == "mul" and extra.shape[0] == 1
    if shared_h:
        in_specs.append(pl.BlockSpec((1, m, PAIR_W), lambda i: (0, 0, 0)))
        args.append(extra)
    elif epilogue == "mul":
        hsteps = extra.shape[0] // nb
        in_specs.append(pl.BlockSpec((nb, m, PAIR_W), lambda i: (i % hsteps, 0, 0)))
        args.append(extra)
    elif epilogue == "scale":
        assert extra_period % nb == 0
        in_specs.append(pl.BlockSpec((None, 1, PAIR_W), lambda i: (i * nb // extra_period, 0, 0)))
        args.append(extra)
    return pl.pallas_call(
        functools.partial(_cmatmul_kernel, nb=nb, shared_g=shared_g, shared_h=shared_h, epilogue=epilogue),
        grid=(steps,),
        in_specs=in_specs,
        out_specs=pl.BlockSpec((nb, m, PAIR_W), lambda i: (i, 0, 0)),
        out_shape=jax.ShapeDtypeStruct((b, m, PAIR_W), F32),
        compiler_params=_params("parallel"),
        name="cmatmul_%s" % (epilogue or "plain"),
    )(*args)


def _cplx_cat(m):
    return jnp.asarray(np.concatenate([m.real, m.imag], axis=-1), F32)


def _dft_single(l):
    n = 2 * l
    k = np.arange(n)
    f = np.exp(-2j * np.pi * ((k[:, None] * k[None, :]) % n) / n)
    return _cplx_cat(f), _cplx_cat(f[:, :l]), _cplx_cat(np.conj(f)[:l, :] / n)


FFT_N1 = 64


def _dft_two_stage(l):
    n = 2 * l
    n1 = FFT_N1
    n2 = n // n1
    a1, a2 = np.arange(n1), np.arange(n2)
    ph = (a1[None, None, :] * a1[None, :, None] * n2 + a2[:, None, None] * a1[None, :, None]) % n
    ga = np.exp(-2j * np.pi * ph / n)
    gb = np.exp(-2j * np.pi * ((a2[:, None] * a2[None, :]) % n2) / n2)
    ga_inv = np.conj(ga).transpose(0, 2, 1)[:, :n1 // 2, :] / n
    return dict(n1=n1, n2=n2, ga_full=_cplx_cat(ga), ga_half=_cplx_cat(ga[:, :, :n1 // 2]),
                gb=_cplx_cat(gb), gb_inv=_cplx_cat(np.conj(gb)), ga_inv=_cplx_cat(ga_inv))


def _swap12(a):
    return a.transpose(0, 2, 1, 3)


def _spectra(taps, norm, l):
    n = 2 * l
    zeros = jnp.zeros((n, D_HY), F32)
    data = jnp.stack([jnp.concatenate([taps[:, :D_HY], zeros], axis=1),
                      jnp.concatenate([taps[:, D_HY:], zeros], axis=1)])
    inv = 1.0 / norm[0]
    scale = jnp.stack([jnp.concatenate([inv[:D_HY]] * 2), jnp.concatenate([inv[D_HY:]] * 2)]).reshape(2, 1, PAIR_W)
    if l <= 256:
        g_full, _, _ = _dft_single(l)
        return _cmatmul(g_full, data, 1, extra=scale, epilogue="scale", extra_period=1)
    c = _dft_two_stage(l)
    n1, n2 = c["n1"], c["n2"]
    xt = _swap12(data.reshape(2, n1, n2, PAIR_W)).reshape(2 * n2, n1, PAIR_W)
    a = _cmatmul(c["ga_full"], xt, 8)
    a = _swap12(a.reshape(2, n2, n1, PAIR_W)).reshape(2 * n1, n2, PAIR_W)
    return _cmatmul(c["gb"], a, 8, extra=scale, epilogue="scale", extra_period=n1)


def _fft_conv(x, spec_o, l):
    p = x.shape[0]
    if l <= 256:
        _, g_half, g_inv = _dft_single(l)
        y = _cmatmul(g_half, x, 4, extra=spec_o, epilogue="mul")
        return _cmatmul(g_inv, y, 4)
    c = _dft_two_stage(l)
    n1, n2 = c["n1"], c["n2"]
    xt = _swap12(x.reshape(p, n1 // 2, n2, PAIR_W)).reshape(p * n2, n1 // 2, PAIR_W)
    a = _cmatmul(c["ga_half"], xt, 8)
    a = _swap12(a.reshape(p, n2, n1, PAIR_W)).reshape(p * n1, n2, PAIR_W)
    y = _cmatmul(c["gb"], a, 8, extra=spec_o, epilogue="mul")
    d = _cmatmul(c["gb_inv"], y, 8)
    d = _swap12(d.reshape(p, n1, n2, PAIR_W)).reshape(p * n2, n1, PAIR_W)
    yt = _cmatmul(c["ga_inv"], d, 8)
    return _swap12(yt.reshape(p, n2, n1 // 2, PAIR_W)).reshape(p, l, PAIR_W)


def _hyena(zh, s, l, lp, spectra):
    v, x1, x2 = _hy_prep(zh.reshape(s, l, 3 * D_HY), lp["hy_conv_w"], lp["hy_conv_b"])
    per = spectra.shape[0] // 2
    z = _hy_gate(x1, _fft_conv(v, spectra[:per], l), v, lp["hy_bias"][0], final=False)
    y = _hy_gate(x2, _fft_conv(z, spectra[per:], l), z, lp["hy_bias"][1], final=True)
    return y.reshape(s * l, D_HY)


ATT_SCALE = HEAD_DIM ** -0.5


def _head_mask(h):
    lane = lax.broadcasted_iota(jnp.int32, (1, LANES), 1)
    return (lane // HEAD_DIM) == h


def _softmax_pv(scores, values):
    m = functools.reduce(jnp.maximum, [jnp.max(s, axis=-1, keepdims=True) for s in scores])
    ps = [jnp.exp(s - m) for s in scores]
    denom = functools.reduce(jnp.add, [jnp.sum(p, axis=-1, keepdims=True) for p in ps])
    acc = functools.reduce(jnp.add, [_dot(p.astype(BF16), v) for p, v in zip(ps, values)])
    return acc * (1.0 / denom)


def _attn_ctx_kernel(q_ref, k_ref, v_ref, o_ref, kc_ref, vc_ref):
    q, k, v = q_ref[...], k_ref[...], v_ref[...]
    kb = k.astype(BF16)
    out = jnp.zeros_like(q)
    for h in range(2):
        m = _head_mask(h)
        s = _dot_nt(jnp.where(m, q, 0.0).astype(BF16), kb) * ATT_SCALE
        out += _softmax_pv([s], [jnp.where(m, v, 0.0).astype(BF16)])
    o_ref[...] = out
    kc_ref[0] = k[:, :HEAD_DIM]
    kc_ref[1] = k[:, HEAD_DIM:]
    vc_ref[0] = v[:, :HEAD_DIM]
    vc_ref[1] = v[:, HEAD_DIM:]


def _attn_ctx(qkv, b, l):
    hp = NA_HEADS // 2
    blk = lambda off: pl.BlockSpec((l, LANES), lambda i, j: (i, off + j))
    cache = pl.BlockSpec((None, 2, l, HEAD_DIM), lambda i, j: (i, j, 0, 0))
    cache_shape = jax.ShapeDtypeStruct((b, NA_HEADS, l, HEAD_DIM), F32)
    return pl.pallas_call(
        _attn_ctx_kernel,
        grid=(b, hp),
        in_specs=[blk(0), blk(hp), blk(2 * hp)],
        out_specs=[pl.BlockSpec((l, LANES), lambda i, j: (i, j)), cache, cache],
        out_shape=[jax.ShapeDtypeStruct((b * l, D_NA), F32), cache_shape, cache_shape],
        compiler_params=_params("parallel", "parallel"),
        name="attn_ctx",
    )(qkv, qkv, qkv)


def _rope(x, cos, sin_signed):
    lane = lax.broadcasted_iota(jnp.int32, (1, LANES), 1)
    first = (lane % (2 * ROPE_PAIRS)) < ROPE_PAIRS
    partner = jnp.where(first, pltpu.roll(x, LANES - ROPE_PAIRS, 1), pltpu.roll(x, ROPE_PAIRS, 1))
    return x * cos + partner * sin_signed


ROPE_BLK = 512


def _attn_lat_kernel(q_ref, k_ref, v_ref, kc_ref, vc_ref, bias_ref, cos_ref, sin_ref, o_ref,
                     krot_scr, vm_scr, kcp_scr, vcp_scr, *, rows):
    t = q_ref.shape[0]
    m0 = _head_mask(0)
    for r0 in range(0, t, ROPE_BLK):
        sl = slice(r0, r0 + ROPE_BLK)
        kr = _rope(k_ref[sl, :], cos_ref[sl, :], sin_ref[sl, :])
        vv = v_ref[sl, :]
        krot_scr[0, sl, :] = jnp.where(m0, kr, 0.0).astype(BF16)
        krot_scr[1, sl, :] = jnp.where(m0, 0.0, kr).astype(BF16)
        vm_scr[0, sl, :] = jnp.where(m0, vv, 0.0).astype(BF16)
        vm_scr[1, sl, :] = jnp.where(m0, 0.0, vv).astype(BF16)
    zpad = jnp.zeros((kc_ref.shape[1], HEAD_DIM), F32)
    kcp_scr[0] = jnp.concatenate([kc_ref[0], zpad], axis=1).astype(BF16)
    kcp_scr[1] = jnp.concatenate([zpad, kc_ref[1]], axis=1).astype(BF16)
    vcp_scr[0] = jnp.concatenate([vc_ref[0], zpad], axis=1).astype(BF16)
    vcp_scr[1] = jnp.concatenate([zpad, vc_ref[1]], axis=1).astype(BF16)

    def row_body(r, carry):
        rs = jnp.clip(r - WIN_R // 2, 0, rows - WIN_R)
        q0 = pl.multiple_of(r * GRID_W, GRID_W)
        k0 = pl.multiple_of(rs * GRID_W, GRID_W)
        q = q_ref[pl.ds(q0, GRID_W), :]
        q_raw = q.astype(BF16)
        q_rot = _rope(q, cos_ref[pl.ds(q0, GRID_W), :], sin_ref[pl.ds(q0, GRID_W), :]).astype(BF16)
        start = rs - r + (WIN_R - 1)
        out = jnp.zeros((GRID_W, LANES), F32)
        for h in range(2):
            s_win = _dot_nt(q_rot, krot_scr[h, pl.ds(k0, WIN_R * GRID_W), :]) * ATT_SCALE + bias_ref[h, start]
            s_ctx = _dot_nt(q_raw, kcp_scr[h]) * ATT_SCALE
            out += _softmax_pv([s_win, s_ctx], [vm_scr[h, pl.ds(k0, WIN_R * GRID_W), :], vcp_scr[h]])
        o_ref[pl.ds(q0, GRID_W), :] = out
        return carry

    lax.fori_loop(0, rows, row_body, 0)


def _attn_lat(qkv, cache_k, cache_v, layer, bias, cos, sin_signed, b, t):
    hp = NA_HEADS // 2
    rows = t // GRID_W
    past = cache_k.shape[3]
    blk = lambda off: pl.BlockSpec((t, LANES), lambda i, j: (i, off + j))
    cache = pl.BlockSpec((None, None, 2, past, HEAD_DIM), lambda i, j: (i, layer, j, 0, 0))
    table = pl.BlockSpec((t, LANES), lambda i, j: (0, 0))
    return pl.pallas_call(
        functools.partial(_attn_lat_kernel, rows=rows),
        grid=(b, hp),
        in_specs=[blk(0), blk(hp), blk(2 * hp), cache, cache,
                  pl.BlockSpec((2, WIN_R, GRID_W, WIN_R * GRID_W), lambda i, j: (j, 0, 0, 0)), table, table],
        out_specs=pl.BlockSpec((t, LANES), lambda i, j: (i, j)),
        out_shape=jax.ShapeDtypeStruct((b * t, D_NA), F32),
        scratch_shapes=[pltpu.VMEM((2, t, LANES), BF16), pltpu.VMEM((2, t, LANES), BF16),
                        pltpu.VMEM((2, past, LANES), BF16), pltpu.VMEM((2, past, LANES), BF16)],
        compiler_params=_params("parallel", "parallel"),
        name="attn_lat",
    )(qkv, qkv, qkv, cache_k, cache_v, bias, cos, sin_signed)


def _rope_tables(t):
    pos = np.arange(t)
    row = jnp.asarray(pos // GRID_W, F32)
    col = jnp.asarray(pos % GRID_W, F32)
    inv = ROPE_BASE ** (-jnp.arange(ROPE_PAIRS, dtype=F32) / ROPE_PAIRS)
    ang = jnp.concatenate([row[:, None] * inv] * 2 + [col[:, None] * inv] * 2, axis=-1)
    sign = jnp.asarray(np.where((np.arange(HEAD_DIM) % (2 * ROPE_PAIRS)) < ROPE_PAIRS, -1.0, 1.0), F32)
    cos = jnp.concatenate([jnp.cos(ang)] * 2, axis=-1)
    sin_signed = jnp.concatenate([jnp.sin(ang) * sign] * 2, axis=-1)
    return cos, sin_signed


def _bias_table(rpb):
    qc = np.arange(GRID_W)
    cstart = np.clip(qc - WIN_C // 2, 0, GRID_W - WIN_C)
    off = qc[None, :] - qc[:, None]
    valid = (qc[None, :] >= cstart[:, None]) & (qc[None, :] < cstart[:, None] + WIN_C)
    col_idx = np.clip(off, -(WIN_C - 1), WIN_C - 1) + (WIN_C - 1)
    tab = jnp.where(jnp.asarray(valid), rpb.astype(F32)[:, :, col_idx], NEG_BIG)
    per_start = jnp.stack([tab[:, s:s + WIN_R] for s in range(WIN_R)], axis=1)
    return per_start.transpose(0, 1, 3, 2, 4).reshape(NA_HEADS, WIN_R, GRID_W, WIN_R * GRID_W)


def _blk_transpose8(a):
    lane = lax.broadcasted_iota(jnp.int32, a[0].shape, 1)
    blk = lane // S5_CH
    for delta in (4, 2, 1):
        upper = (blk & delta) != 0
        new = list(a)
        for i in range(8):
            if i & delta:
                continue
            lo_arr, hi_arr = a[i], a[i + delta]
            new[i] = jnp.where(upper, pltpu.roll(hi_arr, delta * S5_CH, 1), lo_arr)
            new[i + delta] = jnp.where(upper, hi_arr, pltpu.roll(lo_arr, LANES - delta * S5_CH, 1))
        a = new
    return a


S5_EPI_ROWS = 512


def _s5_kernel(sua_ref, sub_ref, s0_ref, mintra_ref, win_ref, wout_ref, lama_ref, lamb_ref, d_ref, gw_ref, gb_ref,
               y_ref, fin_ref, u_scr, g_scr, s_scr, *, nc, cps):
    su_refs = (sua_ref, sub_ref)

    def flatten(i, carry):
        base = pl.multiple_of(i * (8 * S5_Q), 8 * S5_Q)
        r0 = pl.multiple_of(i * 8, 8)
        for slab in range(2):
            for half in range(2):
                a = [su_refs[slab][pl.ds(base + half * 8 + tt, 8, stride=S5_Q), :] for tt in range(8)]
                bt = _blk_transpose8(a)
                for g in range(8):
                    u_scr[slab * 8 + g, pl.ds(r0, 8), half * LANES:(half + 1) * LANES] = bt[g]
        return carry

    lax.fori_loop(0, nc // 8, flatten, 0)

    for g in range(S5_GROUPS):
        u = u_scr[g]
        for e in range(2):
            g_scr[e, pl.ds(g, nc, stride=S5_GROUPS), :] = _dot2(u, win_ref[e, g])

    def cmul_add(s, e, inc):
        return s * lama_ref[e] + pltpu.roll(s, S5_STATE, 1) * lamb_ref[e] + inc

    def recur(i, carry):
        sf, sb = carry
        kf = i % cps
        seq_f = i // cps
        sf = jnp.where(kf == 0, s0_ref[seq_f, 0], sf)
        row_f = pl.multiple_of(i * S5_GROUPS, S5_GROUPS)
        s_scr[0, pl.ds(row_f, S5_GROUPS), :] = sf
        sf = cmul_add(sf, 0, g_scr[0, pl.ds(row_f, S5_GROUPS), :])

        @pl.when(kf == cps - 1)
        def _():
            fin_ref[seq_f, 0] = sf

        ib = nc - 1 - i
        kb = ib % cps
        seq_b = ib // cps
        sb = jnp.where(kb == cps - 1, s0_ref[seq_b, 1], sb)
        row_b = pl.multiple_of(ib * S5_GROUPS, S5_GROUPS)
        s_scr[1, pl.ds(row_b, S5_GROUPS), :] = sb
        sb = cmul_add(sb, 1, g_scr[1, pl.ds(row_b, S5_GROUPS), :])

        @pl.when(kb == 0)
        def _():
            fin_ref[seq_b, 1] = sb

        return sf, sb

    zero = jnp.zeros((S5_GROUPS, LANES), F32)
    lax.fori_loop(0, nc, recur, (zero, zero))

    for g in range(S5_GROUPS):
        acc = _dot2(u_scr[g], mintra_ref[g])
        for e in range(2):
            acc += _dot2(s_scr[e, pl.ds(g, nc, stride=S5_GROUPS), :], wout_ref[e, g])
        u_scr[g] = acc

    def unflatten(i, carry):
        base = pl.multiple_of(i * (8 * S5_Q), 8 * S5_Q)
        r0 = pl.multiple_of(i * 8, 8)
        for slab in range(2):
            for half in range(2):
                bt = [u_scr[slab * 8 + g, pl.ds(r0, 8), half * LANES:(half + 1) * LANES] for g in range(8)]
                a = _blk_transpose8(bt)
                for tt in range(8):
                    g_scr[slab, pl.ds(base + half * 8 + tt, 8, stride=S5_Q), :] = a[tt]
        return carry

    lax.fori_loop(0, nc // 8, unflatten, 0)

    gw = gw_ref[...].astype(BF16)
    for r0 in range(0, nc * S5_Q, S5_EPI_ROWS):
        sl = slice(r0, r0 + S5_EPI_ROWS)
        u = jnp.concatenate([sua_ref[sl, :], sub_ref[sl, :]], axis=1)
        y = jnp.concatenate([g_scr[0, sl, :], g_scr[1, sl, :]], axis=1) + d_ref[...] * u
        y = 0.5 * y * (1.0 + jnp.tanh(math.sqrt(2.0 / math.pi) * (y + 0.044715 * (y * y * y))))
        z = _dot(y.astype(BF16), gw) + gb_ref[...]
        y_ref[sl, :] = y / (1.0 + jnp.exp(-z))


def _s5_matrices(lp):
    hi = lax.Precision.HIGHEST
    lam = lax.complex(lp["s5_a_re"].astype(F32), lp["s5_a_im"].astype(F32))
    dt = jnp.exp(lp["s5_log_dt"].astype(F32))[..., None]
    b_mat = lax.complex(lp["s5_b_re"].astype(F32), lp["s5_b_im"].astype(F32))
    c_mat = lax.complex(lp["s5_c_re"].astype(F32), lp["s5_c_im"].astype(F32))
    lam_bar = jnp.exp(lam * dt)
    b_bar = ((lam_bar - 1.0) / lam)[..., None] * b_mat
    steps = jnp.arange(S5_Q + 1, dtype=F32)
    pw = jnp.exp((lam * dt)[None] * steps[:, None, None, None].astype(jnp.complex64))
    q = S5_Q
    ts = np.arange(q)
    pw_in = jnp.stack([pw[q - 1 - ts, 0], pw[ts, 1]], axis=0)
    win = jnp.einsum("etgp,egpc->egtcp", pw_in, b_bar, precision=hi).reshape(2, S5_GROUPS, q * S5_CH, S5_STATE)
    win = jnp.concatenate([win.real, win.imag], axis=-1)
    pw_out = jnp.stack([pw[ts + 1, 0], pw[q - ts, 1]], axis=0)
    wout = jnp.einsum("etgp,egcp->egptc", pw_out, c_mat, precision=hi).reshape(2, S5_GROUPS, S5_STATE, q * S5_CH)
    wout = jnp.concatenate([wout.real, -wout.imag], axis=-2)
    kern = jnp.einsum("egcp,degp,egpk->edgck", c_mat, pw[:q], b_bar, precision=hi).real
    lag = ts[None, :] - ts[:, None]
    fwd = jnp.where(jnp.asarray(lag >= 0)[:, :, None, None, None], kern[0][np.clip(lag, 0, q - 1)], 0.0)
    bwd = jnp.where(jnp.asarray(lag <= 0)[:, :, None, None, None], kern[1][np.clip(-lag, 0, q - 1)], 0.0)
    mintra = (fwd + bwd).transpose(2, 0, 4, 1, 3).reshape(S5_GROUPS, q * S5_CH, q * S5_CH)
    lam_q = pw[q]
    lama = jnp.concatenate([lam_q.real, lam_q.real], axis=-1)
    lamb = jnp.concatenate([-lam_q.imag, lam_q.imag], axis=-1)
    return mintra.astype(BF16), win.astype(BF16), wout.astype(BF16), lama, lamb


def _s5(su_a, su_b, s0, mats, lp, seqs_per_tile, seq_len):
    n = su_a.shape[0]
    mintra, win, wout, lama, lamb = mats
    tile = seqs_per_tile * seq_len
    nc = tile // S5_Q
    nseq = n // seq_len
    row = lambda width: pl.BlockSpec((tile, width), lambda i: (i, 0))
    state = pl.BlockSpec((seqs_per_tile, 2, S5_GROUPS, LANES), lambda i: (i, 0, 0, 0))
    full = lambda a: pl.BlockSpec(a.shape, lambda i: (0,) * a.ndim)
    d = lp["s5_d"].reshape(1, D_S5)
    gb = lp["s5_glu_b"].reshape(1, D_S5)
    return pl.pallas_call(
        functools.partial(_s5_kernel, nc=nc, cps=seq_len // S5_Q),
        grid=(n // tile,),
        in_specs=[row(LANES), row(LANES), state, full(mintra), full(win), full(wout), full(lama), full(lamb),
                  full(d), full(lp["s5_glu_w"]), full(gb)],
        out_specs=[row(D_S5), state],
        out_shape=[jax.ShapeDtypeStruct((n, D_S5), F32), jax.ShapeDtypeStruct((nseq, 2, S5_GROUPS, LANES), F32)],
        scratch_shapes=[pltpu.VMEM((S5_GROUPS, nc, 2 * LANES), F32), pltpu.VMEM((2, nc * S5_GROUPS, LANES), F32),
                        pltpu.VMEM((2, nc * S5_GROUPS, LANES), F32)],
        compiler_params=_params("parallel"),
        name="s5",
    )(su_a, su_b, s0, mintra, win, wout, lama, lamb, d, lp["s5_glu_w"], gb)


_LAYER_KEYS = ("norm_g", "in_w", "out_w", "hy_conv_w", "hy_conv_b", "hy_f_w1", "hy_f_b1", "hy_f_w2", "hy_f_b2",
               "hy_f_freq", "hy_f_w3", "hy_decay", "hy_bias", "na_rpb", "s5_a_re", "s5_a_im", "s5_log_dt",
               "s5_b_re", "s5_b_im", "s5_c_re", "s5_c_im", "s5_d", "s5_glu_w", "s5_glu_b")
S5_CTX_SEQS_PER_TILE = 8


def kernel(x_prompt, x_sample, cache_k, cache_v, state_s5_re, state_s5_im, c, c_ctx, norm_g, ada_w, ada_b, in_w, out_w, hy_conv_w, hy_conv_b, hy_f_w1, hy_f_b1, hy_f_w2, hy_f_b2, hy_f_freq, hy_f_w3, hy_decay, hy_bias, na_rpb, s5_a_re, s5_a_im, s5_log_dt, s5_b_re, s5_b_im, s5_c_re, s5_c_im, s5_d, s5_glu_w, s5_glu_b, final_norm_g):
    weights = dict(norm_g=norm_g, in_w=in_w, out_w=out_w, hy_conv_w=hy_conv_w, hy_conv_b=hy_conv_b, hy_f_w1=hy_f_w1,
                   hy_f_b1=hy_f_b1, hy_f_w2=hy_f_w2, hy_f_b2=hy_f_b2, hy_f_freq=hy_f_freq, hy_f_w3=hy_f_w3,
                   hy_decay=hy_decay, hy_bias=hy_bias, na_rpb=na_rpb, s5_a_re=s5_a_re, s5_a_im=s5_a_im,
                   s5_log_dt=s5_log_dt, s5_b_re=s5_b_re, s5_b_im=s5_b_im, s5_c_re=s5_c_re, s5_c_im=s5_c_im,
                   s5_d=s5_d, s5_glu_w=s5_glu_w, s5_glu_b=s5_glu_b)
    bc, lc, _ = x_prompt.shape
    bl, ll, _ = x_sample.shape
    cond = jnp.zeros((8, D_MODEL), F32).at[0].set(c_ctx).at[1:1 + bl].set(c)
    mod = _modulation(cond, ada_w, ada_b)
    cos, sin_signed = _rope_tables(ll)
    xc = x_prompt.reshape(bc * lc, D_MODEL)
    xl = x_sample.reshape(bl * ll, D_MODEL)
    s0_ctx = jnp.zeros((bc, 2, S5_GROUPS, LANES), F32)
    new_k, new_v, new_fin = [], [], []
    for li in range(DEPTH):
        lp = {key: weights[key][li] for key in _LAYER_KEYS}
        last = li == DEPTH - 1
        mod3 = mod[li, :1 + bl].reshape(1 + bl, 1, 3 * D_MODEL)
        w_in = lp["in_w"].astype(BF16)
        w_out = lp["out_w"].astype(BF16)
        s5_mats = _s5_matrices(lp)
        filt = (lp["hy_f_w1"], lp["hy_f_b1"], lp["hy_f_w2"], lp["hy_f_b2"], lp["hy_f_freq"], lp["hy_f_w3"], lp["hy_decay"])

        zh, qkv, su_a, su_b, gp = _in_proj(xc, mod3, lp["norm_g"], w_in, 0, lc)
        y_hy = _hyena(zh, bc, lc, lp, _spectra(*_hy_filter(lc, *filt), lc))
        y_na, kc, vc = _attn_ctx(qkv, bc, lc)
        y_s5, fin = _s5(su_a, su_b, s0_ctx, s5_mats, lp, S5_CTX_SEQS_PER_TILE, lc)
        xc = _out_proj(xc, y_hy, y_na, y_s5, gp, mod3, w_out, final_norm_g, 0, lc, last)
        new_k.append(kc)
        new_v.append(vc)
        new_fin.append(fin)

        zh, qkv, su_a, su_b, gp = _in_proj(xl, mod3, lp["norm_g"], w_in, 1, ll)
        y_hy = _hyena(zh, bl, ll, lp, _spectra(*_hy_filter(ll, *filt), ll))
        y_na = _attn_lat(qkv, cache_k, cache_v, li, _bias_table(lp["na_rpb"]), cos, sin_signed, bl, ll)
        s0_lat = jnp.concatenate([state_s5_re[:, li], state_s5_im[:, li]], axis=-1)
        y_s5, _ = _s5(su_a, su_b, s0_lat, s5_mats, lp, 1, ll)
        xl = _out_proj(xl, y_hy, y_na, y_s5, gp, mod3, w_out, final_norm_g, 1, ll, last)

    fin = jnp.stack(new_fin, axis=1)
    return (xc.reshape(bc, lc, D_MODEL), xl.reshape(bl, ll, D_MODEL),
            jnp.stack(new_k, axis=1), jnp.stack(new_v, axis=1),
            fin[..., :S5_STATE], fin[..., S5_STATE:])
```
